```python
import jax, jax.numpy as jnp
from jax import lax
import numpy as np

D_MODEL = 2048
BATCH = 32
SEQ = 256
DEPTH = 2
DEC_BATCH = 2
DEC_SEQ = 4096
PAST_LEN = 512

GRID_W = 64
N_EVEN = (DEPTH + 1) // 2
N_ODD = DEPTH // 2
CHUNK = 128
NORM_EPS = 1e-6
D_A = D_MODEL // 2
NB_A = 8
BS_A = D_A // NB_A
CONV_W = 4
CONV_LEFT = 2
RG_C = 8.0
H_B = 4
DK_B = 256
DV_B = 256
QK_B = H_B * DK_B
V_B = H_B * DV_B
ROPE_BASE = 10000.0
H_C = 8
DK_C = 128
DV_C = 256
QK_C = H_C * DK_C
V_C = H_C * DV_C
IN0 = 2 * D_A + 2 * QK_B + 2 * V_B
SPLIT0 = (D_A, 2 * D_A, 2 * D_A + QK_B, 2 * D_A + 2 * QK_B, 2 * D_A + 2 * QK_B + V_B)
MIX0 = D_A + V_B
IN1 = 2 * QK_C + 2 * V_C + 4 * H_C
SPLIT1 = (QK_C, 2 * QK_C, 2 * QK_C + V_C, 2 * QK_C + 2 * V_C)
MIX1 = V_C
N_GROUPS = 4
E_PER_GROUP = 8
N_EXPERTS = N_GROUPS * E_PER_GROUP
TOP_K = 2
D_EXP = 1024
MOE_BLOCK = 128

kernel_name = "hybrid_rglru_retention_mlstm_hmoe_dit_step"


def modulated_norm(x, g, shift, scale):
    xf = x.astype(jnp.float32)
    xn = xf * lax.rsqrt(jnp.mean(xf * xf, axis=-1, keepdims=True) + NORM_EPS) * g.astype(jnp.float32)
    return (xn * (1.0 + scale) + shift).astype(x.dtype)


def rms_norm(x, g):
    xf = x.astype(jnp.float32)
    xn = xf * lax.rsqrt(jnp.mean(xf * xf, axis=-1, keepdims=True) + NORM_EPS) * g.astype(jnp.float32)
    return xn.astype(x.dtype)


def head_rms(o):
    return o * lax.rsqrt(jnp.mean(o * o, axis=-1, keepdims=True) + NORM_EPS)


def short_conv(x, w, b):
    s = x.shape[1]
    xp = jnp.pad(x, ((0, 0), (CONV_LEFT, CONV_W - 1 - CONV_LEFT), (0, 0)))
    return sum(xp[:, t:t + s] * w[t] for t in range(CONV_W)) + b


def linear_scan(a, u, h0):
    def comb(e1, e2):
        a1, b1 = e1
        a2, b2 = e2
        return a1 * a2, a2 * b1 + b2
    acum, bcum = lax.associative_scan(comb, (a, u), axis=1)
    return acum * h0[:, None] + bcum


def axial_rope(x):
    s = x.shape[1]
    rows = s // GRID_W
    row_id = jnp.repeat(jnp.arange(rows), GRID_W).astype(jnp.float32)
    col_id = (jnp.arange(rows * GRID_W) % GRID_W).astype(jnp.float32)
    n_f = x.shape[-1] // 4
    freqs = ROPE_BASE ** (-jnp.arange(n_f, dtype=jnp.float32) / n_f)
    ang = jnp.concatenate([row_id[:, None] * freqs, col_id[:, None] * freqs], axis=-1)
    cos = jnp.cos(ang)[None, :, None, :]
    sin = jnp.sin(ang)[None, :, None, :]
    x1 = x[..., 0::2]
    x2 = x[..., 1::2]
    return jnp.stack([x1 * cos - x2 * sin, x1 * sin + x2 * cos], axis=-1).reshape(x.shape)


def rglru_mixer(xa, ga, conv_w, conv_b, wr, br, wi, bi, lam, h0):
    f32 = jnp.float32
    xc = short_conv(xa.astype(f32), conv_w.astype(f32), conv_b.astype(f32))
    bsz, s, _ = xc.shape
    xblk = xc.reshape(bsz, s, NB_A, BS_A)
    r = jax.nn.sigmoid(jnp.einsum('bsnk,dnkj->dbsnj', xblk, wr.astype(f32)).reshape(2, bsz, s, D_A)
                       + br.astype(f32)[:, None, None])
    gi = jax.nn.sigmoid(jnp.einsum('bsnk,dnkj->dbsnj', xblk, wi.astype(f32)).reshape(2, bsz, s, D_A)
                        + bi.astype(f32)[:, None, None])
    log_a = -RG_C * r * jax.nn.softplus(-lam.astype(f32))[:, None, None]
    a = jnp.exp(log_a)
    u = jnp.sqrt(-jnp.expm1(2.0 * log_a)) * gi * xc[None]
    h0 = h0.astype(f32)
    h_f = linear_scan(a[0], u[0], h0[:, 0])
    h_b = linear_scan(a[1, :, ::-1], u[1, :, ::-1], h0[:, 1])
    y = (h_f + h_b[:, ::-1]) * jax.nn.gelu(ga.astype(f32))
    return y, jnp.stack([h_f[:, -1], h_b[:, -1]], axis=1)


def retention_chunked(q, k, v, log_gamma, s0):
    bsz, s, h, dk = q.shape
    dv = v.shape[-1]
    n = s // CHUNK
    q = q.reshape(bsz, n, CHUNK, h, dk)
    k = k.reshape(bsz, n, CHUNK, h, dk)
    v = v.reshape(bsz, n, CHUNK, h, dv)
    idx = jnp.arange(CHUNK, dtype=jnp.float32)
    diff = idx[:, None] - idx[None, :]
    decay_mask = jnp.where(diff[None] >= 0,
                           jnp.exp(jnp.maximum(diff, 0.0)[None] * log_gamma[:, None, None]), 0.0)
    scores = jnp.einsum('bnihd,bnjhd->bnhij', q, k) * decay_mask
    intra = jnp.einsum('bnhij,bnjhe->bnihe', scores, v)
    k_dec = jnp.exp((CHUNK - 1 - idx)[:, None] * log_gamma)
    kv = jnp.einsum('bnjhd,jh,bnjhe->bnhde', k, k_dec, v)
    chunk_dec = jnp.exp(CHUNK * log_gamma)[:, None, None]

    def step(st, kv_c):
        return chunk_dec * st + kv_c, st

    s_last, s_prev = lax.scan(step, s0, jnp.moveaxis(kv, 1, 0))
    s_prev = jnp.moveaxis(s_prev, 0, 1)
    q_dec = jnp.exp((idx + 1.0)[:, None] * log_gamma)
    inter = jnp.einsum('bnihd,ih,bnhde->bnihe', q, q_dec, s_prev)
    return (intra + inter).reshape(bsz, s, h, dv), s_last


def retention_mixer(q, k, v, g, decay_p, norm_g, s0, latent):
    f32 = jnp.float32
    bsz, s, _ = q.shape
    q = q.astype(f32).reshape(bsz, s, H_B, DK_B)
    k = k.astype(f32).reshape(bsz, s, H_B, DK_B) * DK_B ** -0.5
    v = v.astype(f32).reshape(bsz, s, H_B, DV_B)
    if latent:
        q = axial_rope(q)
        k = axial_rope(k)
    log_gamma = jnp.log1p(-jnp.exp(decay_p.astype(f32)))
    s0 = s0.astype(f32)
    o_f, s_f = retention_chunked(q, k, v, log_gamma[0], s0[:, 0])
    o_b, s_b = retention_chunked(q[:, ::-1], k[:, ::-1], v[:, ::-1], log_gamma[1], s0[:, 1])
    o = head_rms(o_f + o_b[:, ::-1]) * norm_g.astype(f32).reshape(H_B, DV_B)
    y = jax.nn.silu(g.astype(f32)) * o.reshape(bsz, s, V_B)
    return y, jnp.stack([s_f, s_b], axis=1)


def mlstm_chunked(q, k, v, i_pre, f_pre, c0, n0, m0):
    bsz, s, h, dk = q.shape
    dv = v.shape[-1]
    n = s // CHUNK
    q = q.reshape(bsz, n, CHUNK, h, dk)
    k = k.reshape(bsz, n, CHUNK, h, dk)
    v = v.reshape(bsz, n, CHUNK, h, dv)
    ig = i_pre.reshape(bsz, n, CHUNK, h)
    b = jnp.cumsum(jax.nn.log_sigmoid(f_pre).reshape(bsz, n, CHUNK, h), axis=2)
    b_tot = b[:, :, -1]
    a = b_tot[:, :, None] - b + ig
    m_chunk = jnp.max(a, axis=2)
    w = jnp.exp(a - m_chunk[:, :, None])
    kv = jnp.einsum('bnjh,bnjhd,bnjhe->bnhde', w, k, v)
    ks = jnp.einsum('bnjh,bnjhd->bnhd', w, k)

    def step(carry, inp):
        cm, nv, m = carry
        kv_c, ks_c, mc, bt = inp
        m_new = jnp.maximum(bt + m, mc)
        s_old = jnp.exp(bt + m - m_new)
        s_new = jnp.exp(mc - m_new)
        c_new = s_old[..., None, None] * cm + s_new[..., None, None] * kv_c
        n_new = s_old[..., None] * nv + s_new[..., None] * ks_c
        return (c_new, n_new, m_new), (cm, nv, m)

    xs = (jnp.moveaxis(kv, 1, 0), jnp.moveaxis(ks, 1, 0), jnp.moveaxis(m_chunk, 1, 0), jnp.moveaxis(b_tot, 1, 0))
    (c_f, n_f, m_f), (c_p, n_p, m_p) = lax.scan(step, (c0, n0, m0), xs)
    c_p = jnp.moveaxis(c_p, 0, 1)
    n_p = jnp.moveaxis(n_p, 0, 1)
    m_p = jnp.moveaxis(m_p, 0, 1)
    idx = jnp.arange(CHUNK)
    causal = (idx[:, None] >= idx[None, :])[None, None, :, :, None]
    dlog = jnp.where(causal, b[:, :, :, None, :] - b[:, :, None, :, :] + ig[:, :, None, :, :], -jnp.inf)
    g = b + m_p[:, :, None, :]
    m_i = jnp.maximum(g, jnp.max(dlog, axis=3))
    sc = jnp.einsum('bnihd,bnjhd->bnijh', q, k) * jnp.exp(dlog - m_i[:, :, :, None, :])
    sg = jnp.exp(g - m_i)
    num = jnp.einsum('bnijh,bnjhe->bnihe', sc, v) + jnp.einsum('bnihd,bnhde->bnihe', q, c_p) * sg[..., None]
    den = jnp.sum(sc, axis=3) + jnp.einsum('bnihd,bnhd->bnih', q, n_p) * sg
    hout = num / jnp.maximum(jnp.abs(den), jnp.exp(-m_i))[..., None]
    return hout.reshape(bsz, s, h, dv), c_f, n_f, m_f


def mlstm_mixer(q, k, v, o, gts, gate_b, norm_g, c0, n0, m0):
    f32 = jnp.float32
    bsz, s, _ = q.shape
    q = q.astype(f32).reshape(bsz, s, H_C, DK_C) * DK_C ** -0.5
    k = k.astype(f32).reshape(bsz, s, H_C, DK_C)
    v = v.astype(f32).reshape(bsz, s, H_C, DV_C)
    gp = (gts.astype(f32) + gate_b.astype(f32)).reshape(bsz, s, 4, H_C)
    c0 = c0.astype(f32)
    n0 = n0.astype(f32)
    m0 = m0.astype(f32)
    h_f, cf, nf, mf = mlstm_chunked(q, k, v, gp[:, :, 0], gp[:, :, 1], c0[:, 0], n0[:, 0], m0[:, 0])
    h_b, cb, nb, mb = mlstm_chunked(q[:, ::-1], k[:, ::-1], v[:, ::-1], gp[:, ::-1, 2], gp[:, ::-1, 3],
                                    c0[:, 1], n0[:, 1], m0[:, 1])
    hh = head_rms(h_f + h_b[:, ::-1]) * norm_g.astype(f32).reshape(H_C, DV_C)
    y = jax.nn.sigmoid(o.astype(f32)) * hh.reshape(bsz, s, V_C)
    return y, jnp.stack([cf, cb], axis=1), jnp.stack([nf, nb], axis=1), jnp.stack([mf, mb], axis=1)


def hier_moe(x, w_group, w_router, w_gu, w_down):
    f32 = jnp.float32
    bsz, s, d = x.shape
    t = bsz * s
    xt = x.reshape(t, d)
    g_logits = (xt @ w_group).astype(f32)
    g_prob = jax.nn.softmax(g_logits, axis=-1)
    g_idx = jnp.argmax(g_logits, axis=-1)
    p_g = jnp.take_along_axis(g_prob, g_idx[:, None], axis=1)[:, 0]
    e_logits = (xt @ w_router).astype(f32).reshape(t, N_GROUPS, E_PER_GROUP)
    e_logits = jnp.take_along_axis(e_logits, g_idx[:, None, None], axis=1)[:, 0]
    top_p, top_i = lax.top_k(jax.nn.softmax(e_logits, axis=-1), TOP_K)
    weights = p_g[:, None] * top_p / jnp.sum(top_p, axis=-1, keepdims=True)
    expert = g_idx[:, None] * E_PER_GROUP + top_i
    n_asg = t * TOP_K
    flat_e = expert.reshape(n_asg)
    flat_t = jnp.repeat(jnp.arange(t), TOP_K)
    flat_w = weights.reshape(n_asg)
    order = jnp.argsort(flat_e)
    se = flat_e[order]
    st = flat_t[order]
    sw = flat_w[order]
    counts = jnp.bincount(flat_e, length=N_EXPERTS)
    start = jnp.cumsum(counts) - counts
    pcounts = (counts + MOE_BLOCK - 1) // MOE_BLOCK * MOE_BLOCK
    pend = jnp.cumsum(pcounts)
    pstart = pend - pcounts
    dest = pstart[se] + jnp.arange(n_asg) - start[se]
    n_blocks = -(-n_asg // MOE_BLOCK) + N_EXPERTS
    n_rows = n_blocks * MOE_BLOCK
    row_tok = jnp.zeros((n_rows,), jnp.int32).at[dest].set(st)
    row_valid = jnp.zeros((n_rows,), x.dtype).at[dest].set(1)
    block_e = jnp.minimum(jnp.searchsorted(pend, jnp.arange(n_blocks) * MOE_BLOCK, side='right'), N_EXPERTS - 1)
    xb = (xt[row_tok] * row_valid[:, None]).reshape(n_blocks, MOE_BLOCK, d)

    def expert_block(args):
        xblk, e = args
        gate, up = jnp.split(xblk @ w_gu[e], 2, axis=-1)
        return (jax.nn.silu(gate) * up) @ w_down[e]

    yb = lax.map(expert_block, (xb, block_e)).reshape(n_rows, d)
    y = jnp.zeros((t, d), f32).at[st].add(yb[dest].astype(f32) * sw[:, None])
    return y.astype(x.dtype).reshape(bsz, s, d)


def run_stream(x, cvec, latent, rg_h0, ret_s0, ml_c0, ml_n0, ml_m0,
               ada_w, ada_b, norm1_g, norm2_g, in0_w, out0_w, rg_conv_w, rg_conv_b,
               rg_wr, rg_br, rg_wi, rg_bi, rg_lambda, ret_decay, ret_norm_g,
               in1_w, out1_w, ml_gate_b, ml_norm_g, moe_wg, moe_wr, moe_wgu, moe_wd, final_g):
    f32 = jnp.float32
    sc = jax.nn.silu(cvec.astype(f32))
    rg_new, ret_new, c_new, n_new, m_new = [], [], [], [], []
    for l in range(DEPTH):
        mod = sc @ ada_w[l].astype(f32) + ada_b[l].astype(f32)
        sh1, sc1, g1, sh2, sc2, g2 = jnp.split(mod[:, None, :], 6, axis=-1)
        h = modulated_norm(x, norm1_g[l], sh1, sc1)
        j = l // 2
        if l % 2 == 0:
            xa, ga, q, k, v, g = jnp.split(h @ in0_w[j], SPLIT0, axis=-1)
            ya, st_a = rglru_mixer(xa, ga, rg_conv_w[j], rg_conv_b[j], rg_wr[j], rg_br[j],
                                   rg_wi[j], rg_bi[j], rg_lambda[j], rg_h0[:, j])
            yb, st_b = retention_mixer(q, k, v, g, ret_decay[j], ret_norm_g[j], ret_s0[:, j], latent)
            mix = jnp.concatenate([ya, yb], axis=-1).astype(x.dtype) @ out0_w[j]
            rg_new.append(st_a)
            ret_new.append(st_b)
        else:
            q, k, v, o, gts = jnp.split(h @ in1_w[j], SPLIT1, axis=-1)
            yc, st_c, st_n, st_m = mlstm_mixer(q, k, v, o, gts, ml_gate_b[j], ml_norm_g[j],
                                               ml_c0[:, j], ml_n0[:, j], ml_m0[:, j])
            mix = yc.astype(x.dtype) @ out1_w[j]
            c_new.append(st_c)
            n_new.append(st_n)
            m_new.append(st_m)
        x = x + (g1 * mix.astype(f32)).astype(x.dtype)
        h = modulated_norm(x, norm2_g[l], sh2, sc2)
        x = x + (g2 * hier_moe(h, moe_wg[l], moe_wr[l], moe_wgu[l], moe_wd[l]).astype(f32)).astype(x.dtype)
    y = rms_norm(x, final_g)
    dt = x.dtype
    return (y, jnp.stack(rg_new, axis=1).astype(dt), jnp.stack(ret_new, axis=1).astype(dt),
            jnp.stack(c_new, axis=1).astype(dt), jnp.stack(n_new, axis=1).astype(dt),
            jnp.stack(m_new, axis=1).astype(dt))


def setup_inputs(seed: int = 0) -> dict:
    key = jax.random.key(seed)
    ks = jax.random.split(key, 40)
    f32 = jnp.float32
    nrm = lambda i, shape, scale: jax.random.normal(ks[i], shape, f32) * scale
    u = jax.random.uniform(ks[30], (N_EVEN, 2, D_A), f32, minval=0.9, maxval=0.999)
    s_lam = u ** (1.0 / RG_C)
    rg_lambda = jnp.log(s_lam) - jnp.log1p(-s_lam)
    ret_decay = (-(5.0 + jnp.arange(H_B, dtype=f32)) * np.float32(np.log(2.0)))[None, None] + nrm(31, (N_EVEN, 2, H_B), 0.05)
    i_bias = nrm(32, (N_ODD, 2, 1, H_C), 0.1)
    f_bias = jnp.linspace(3.0, 6.0, H_C, dtype=f32) + nrm(33, (N_ODD, 2, 1, H_C), 0.1)
    ml_gate_b = jnp.concatenate([i_bias, f_bias], axis=2).reshape(N_ODD, 4 * H_C)
    return {
        "x_prompt": nrm(0, (BATCH, SEQ, D_MODEL), 1.0),
        "x_sample": nrm(1, (DEC_BATCH, DEC_SEQ, D_MODEL), 1.0),
        "state_rglru_h": nrm(2, (DEC_BATCH, N_EVEN, 2, D_A), 0.5),
        "state_ret_s": nrm(3, (DEC_BATCH, N_EVEN, 2, H_B, DK_B, DV_B), 0.5),
        "state_mlstm_C": nrm(4, (DEC_BATCH, N_ODD, 2, H_C, DK_C, DV_C), 0.5),
        "state_mlstm_n": nrm(5, (DEC_BATCH, N_ODD, 2, H_C, DK_C), 0.5),
        "state_mlstm_m": nrm(6, (DEC_BATCH, N_ODD, 2, H_C), 0.5),
        "c": nrm(7, (DEC_BATCH, D_MODEL), 1.0),
        "c_ctx": nrm(8, (D_MODEL,), 1.0),
        "ada_w": nrm(9, (DEPTH, D_MODEL, 6 * D_MODEL), D_MODEL ** -0.5),
        "ada_b": nrm(10, (DEPTH, 6 * D_MODEL), 0.02),
        "norm1_g": 1.0 + nrm(11, (DEPTH, D_MODEL), 0.05),
        "norm2_g": 1.0 + nrm(12, (DEPTH, D_MODEL), 0.05),
        "in0_w": nrm(13, (N_EVEN, D_MODEL, IN0), D_MODEL ** -0.5),
        "out0_w": nrm(14, (N_EVEN, MIX0, D_MODEL), MIX0 ** -0.5),
        "rg_conv_w": nrm(15, (N_EVEN, CONV_W, D_A), CONV_W ** -0.5),
        "rg_conv_b": nrm(16, (N_EVEN, D_A), 0.01),
        "rg_wr": nrm(17, (N_EVEN, 2, NB_A, BS_A, BS_A), BS_A ** -0.5),
        "rg_br": nrm(18, (N_EVEN, 2, D_A), 0.01),
        "rg_wi": nrm(19, (N_EVEN, 2, NB_A, BS_A, BS_A), BS_A ** -0.5),
        "rg_bi": nrm(20, (N_EVEN, 2, D_A), 0.01),
        "rg_lambda": rg_lambda,
        "ret_decay": ret_decay,
        "ret_norm_g": 1.0 + nrm(21, (N_EVEN, V_B), 0.05),
        "in1_w": nrm(22, (N_ODD, D_MODEL, IN1), D_MODEL ** -0.5),
        "out1_w": nrm(23, (N_ODD, MIX1, D_MODEL), MIX1 ** -0.5),
        "ml_gate_b": ml_gate_b,
        "ml_norm_g": 1.0 + nrm(24, (N_ODD, V_C), 0.05),
        "moe_wg": nrm(25, (DEPTH, D_MODEL, N_GROUPS), D_MODEL ** -0.5),
        "moe_wr": nrm(26, (DEPTH, D_MODEL, N_EXPERTS), D_MODEL ** -0.5),
        "moe_wgu": nrm(27, (DEPTH, N_EXPERTS, D_MODEL, 2 * D_EXP), D_MODEL ** -0.5),
        "moe_wd": nrm(28, (DEPTH, N_EXPERTS, D_EXP, D_MODEL), D_EXP ** -0.5),
        "final_g": 1.0 + nrm(29, (D_MODEL,), 0.05),
    }


def reference(x_prompt, x_sample, state_rglru_h, state_ret_s, state_mlstm_C, state_mlstm_n, state_mlstm_m,
              c, c_ctx, ada_w, ada_b, norm1_g, norm2_g, in0_w, out0_w, rg_conv_w, rg_conv_b,
              rg_wr, rg_br, rg_wi, rg_bi, rg_lambda, ret_decay, ret_norm_g,
              in1_w, out1_w, ml_gate_b, ml_norm_g, moe_wg, moe_wr, moe_wgu, moe_wd, final_g):
    f32 = jnp.float32
    bp = x_prompt.shape[0]
    weights = (ada_w, ada_b, norm1_g, norm2_g, in0_w, out0_w, rg_conv_w, rg_conv_b,
               rg_wr, rg_br, rg_wi, rg_bi, rg_lambda, ret_decay, ret_norm_g,
               in1_w, out1_w, ml_gate_b, ml_norm_g, moe_wg, moe_wr, moe_wgu, moe_wd, final_g)
    y_prompt, new_rglru_h, new_ret_s, new_mlstm_C, new_mlstm_n, new_mlstm_m = run_stream(
        x_prompt, c_ctx[None, :], False,
        jnp.zeros((bp,) + state_rglru_h.shape[1:], f32),
        jnp.zeros((bp,) + state_ret_s.shape[1:], f32),
        jnp.zeros((bp,) + state_mlstm_C.shape[1:], f32),
        jnp.zeros((bp,) + state_mlstm_n.shape[1:], f32),
        jnp.zeros((bp,) + state_mlstm_m.shape[1:], f32),
        *weights)
    y_sample = run_stream(x_sample, c, True, state_rglru_h, state_ret_s, state_mlstm_C,
                          state_mlstm_n, state_mlstm_m, *weights)[0]
    return (y_prompt, y_sample, new_rglru_h, new_ret_s, new_mlstm_C, new_mlstm_n, new_mlstm_m)
```

```python
import jax
import jax.numpy as jnp
from jax import lax
from jax.experimental import pallas as pl
from jax.experimental.pallas import tpu as pltpu

F32 = jnp.float32
BF16 = jnp.bfloat16

GRID_W = 64
CHUNK = 128
NORM_EPS = 1e-6
NB_A = 8
CONV_W = 4
CONV_LEFT = 2
RG_C = 8.0
H_B = 4
ROPE_BASE = 10000.0
H_C = 8
N_GROUPS = 4
E_PER_GROUP = 8
N_EXPERTS = N_GROUPS * E_PER_GROUP
TOP_K = 2
VMEM_LIMIT = 56 * 1024 * 1024
MATMUL_TM = 512
MATMUL_TN = 512
EXPERT_TM = 256
EXPERT_TE = 256
LANES = 128


def _mm_kernel(a_ref, w_ref, o_ref):
    o_ref[...] = jnp.dot(a_ref[...].astype(BF16), w_ref[...].astype(BF16), preferred_element_type=F32)


def pmatmul(a, w, tn=MATMUL_TN):
    m, k = a.shape
    n = w.shape[1]
    tm = min(MATMUL_TM, m)
    tn = min(tn, n)
    assert m % tm == 0 and n % tn == 0
    return pl.pallas_call(
        _mm_kernel,
        grid=(m // tm, n // tn),
        in_specs=[pl.BlockSpec((tm, k), lambda i, j: (i, 0)),
                  pl.BlockSpec((k, tn), lambda i, j: (0, j))],
        out_specs=pl.BlockSpec((tm, tn), lambda i, j: (i, j)),
        out_shape=jax.ShapeDtypeStruct((m, n), F32),
        compiler_params=pltpu.CompilerParams(dimension_semantics=("arbitrary", "arbitrary"),
                                             vmem_limit_bytes=VMEM_LIMIT),
        name="proj_matmul",
    )(a, w)


def _expert_kernel(be_ref, x_ref, wg_ref, wu_ref, wd_ref, o_ref, xb_ref):
    j = pl.program_id(1)

    @pl.when(j == 0)
    def _():
        xb_ref[...] = x_ref[...].astype(BF16)

    xb = xb_ref[...]
    gate = jnp.dot(xb, wg_ref[...], preferred_element_type=F32)
    up = jnp.dot(xb, wu_ref[...], preferred_element_type=F32)
    act = ((gate * jax.nn.sigmoid(gate)) * up).astype(BF16)
    part = jnp.dot(act, wd_ref[...], preferred_element_type=F32)

    @pl.when(j == 0)
    def _():
        o_ref[...] = part

    @pl.when(j > 0)
    def _():
        o_ref[...] += part


def moe_experts(xb, block_e, w_gu, w_d, layer):
    r, d = xb.shape
    f = w_d.shape[2]
    tm, te = EXPERT_TM, EXPERT_TE
    assert r % tm == 0 and f % te == 0
    nj = f // te
    grid_spec = pltpu.PrefetchScalarGridSpec(
        num_scalar_prefetch=1,
        grid=(r // tm, nj),
        in_specs=[pl.BlockSpec((tm, d), lambda b, j, be: (b, 0)),
                  pl.BlockSpec((None, None, d, te), lambda b, j, be: (layer, be[b], 0, j)),
                  pl.BlockSpec((None, None, d, te), lambda b, j, be: (layer, be[b], 0, nj + j)),
                  pl.BlockSpec((None, None, te, d), lambda b, j, be: (layer, be[b], j, 0))],
        out_specs=pl.BlockSpec((tm, d), lambda b, j, be: (b, 0)),
        scratch_shapes=[pltpu.VMEM((tm, d), BF16)],
    )
    return pl.pallas_call(
        _expert_kernel,
        grid_spec=grid_spec,
        out_shape=jax.ShapeDtypeStruct((r, d), F32),
        compiler_params=pltpu.CompilerParams(dimension_semantics=("arbitrary", "arbitrary"),
                                             vmem_limit_bytes=VMEM_LIMIT),
        name="moe_experts",
    )(block_e, xb, w_gu, w_gu, w_d)


def _modulated_norm(x, g, shift, scale):
    xn = x * lax.rsqrt(jnp.mean(x * x, axis=-1, keepdims=True) + NORM_EPS) * g
    return xn * (1.0 + scale) + shift


def _head_rms(o):
    return o * lax.rsqrt(jnp.mean(o * o, axis=-1, keepdims=True) + NORM_EPS)


def _short_conv(x, w, b):
    s = x.shape[1]
    xp = jnp.pad(x, ((0, 0), (CONV_LEFT, CONV_W - 1 - CONV_LEFT), (0, 0)))
    return sum(xp[:, t:t + s] * w[t] for t in range(CONV_W)) + b


def _linear_scan(a, u, h0):
    def comb(e1, e2):
        a1, b1 = e1
        a2, b2 = e2
        return a1 * a2, a2 * b1 + b2
    acum, bcum = lax.associative_scan(comb, (a, u), axis=1)
    return acum * h0[:, None] + bcum


def _axial_rope(x):
    s = x.shape[1]
    rows = s // GRID_W
    row_id = jnp.repeat(jnp.arange(rows), GRID_W).astype(F32)
    col_id = (jnp.arange(rows * GRID_W) % GRID_W).astype(F32)
    n_f = x.shape[-1] // 4
    freqs = ROPE_BASE ** (-jnp.arange(n_f, dtype=F32) / n_f)
    ang = jnp.concatenate([row_id[:, None] * freqs, col_id[:, None] * freqs], axis=-1)
    cos = jnp.cos(ang)[None, :, None, :]
    sin = jnp.sin(ang)[None, :, None, :]
    x1 = x[..., 0::2]
    x2 = x[..., 1::2]
    return jnp.stack([x1 * cos - x2 * sin, x1 * sin + x2 * cos], axis=-1).reshape(x.shape)


def _rglru_mixer(xa, ga, conv_w, conv_b, wr, br, wi, bi, lam, h0):
    d_a = xa.shape[-1]
    bs_a = d_a // NB_A
    xc = _short_conv(xa, conv_w, conv_b)
    bsz, s, _ = xc.shape
    xblk = xc.reshape(bsz, s, NB_A, bs_a)
    r = jax.nn.sigmoid(jnp.einsum('bsnk,dnkj->dbsnj', xblk, wr).reshape(2, bsz, s, d_a) + br[:, None, None])
    gi = jax.nn.sigmoid(jnp.einsum('bsnk,dnkj->dbsnj', xblk, wi).reshape(2, bsz, s, d_a) + bi[:, None, None])
    log_a = -RG_C * r * jax.nn.softplus(-lam)[:, None, None]
    a = jnp.exp(log_a)
    u = jnp.sqrt(-jnp.expm1(2.0 * log_a)) * gi * xc[None]
    h_f = _linear_scan(a[0], u[0], h0[:, 0])
    h_b = _linear_scan(a[1, :, ::-1], u[1, :, ::-1], h0[:, 1])
    y = (h_f + h_b[:, ::-1]) * jax.nn.gelu(ga)
    return y, jnp.stack([h_f[:, -1], h_b[:, -1]], axis=1)


def _retention_chunked(q, k, v, log_gamma, s0):
    bsz, s, h, dk = q.shape
    dv = v.shape[-1]
    n = s // CHUNK
    q = q.reshape(bsz, n, CHUNK, h, dk)
    k = k.reshape(bsz, n, CHUNK, h, dk)
    v = v.reshape(bsz, n, CHUNK, h, dv)
    idx = jnp.arange(CHUNK, dtype=F32)
    diff = idx[:, None] - idx[None, :]
    decay_mask = jnp.where(diff[None] >= 0,
                           jnp.exp(jnp.maximum(diff, 0.0)[None] * log_gamma[:, None, None]), 0.0)
    scores = jnp.einsum('bnihd,bnjhd->bnhij', q, k) * decay_mask
    intra = jnp.einsum('bnhij,bnjhe->bnihe', scores, v)
    k_dec = jnp.exp((CHUNK - 1 - idx)[:, None] * log_gamma)
    kv = jnp.einsum('bnjhd,jh,bnjhe->bnhde', k, k_dec, v)
    chunk_dec = jnp.exp(CHUNK * log_gamma)[:, None, None]

    def step(st, kv_c):
        return chunk_dec * st + kv_c, st

    s_last, s_prev = lax.scan(step, s0, jnp.moveaxis(kv, 1, 0))
    s_prev = jnp.moveaxis(s_prev, 0, 1)
    q_dec = jnp.exp((idx + 1.0)[:, None] * log_gamma)
    inter = jnp.einsum('bnihd,ih,bnhde->bnihe', q, q_dec, s_prev)
    return (intra + inter).reshape(bsz, s, h, dv), s_last


def _retention_mixer(q, k, v, g, decay_p, norm_g, s0, latent):
    bsz, s, _ = q.shape
    _, _, h_b, dk_b, dv_b = s0.shape
    q = q.reshape(bsz, s, h_b, dk_b)
    k = k.reshape(bsz, s, h_b, dk_b) * dk_b ** -0.5
    v = v.reshape(bsz, s, h_b, dv_b)
    if latent:
        q = _axial_rope(q)
        k = _axial_rope(k)
    log_gamma = jnp.log1p(-jnp.exp(decay_p))
    o_f, s_f = _retention_chunked(q, k, v, log_gamma[0], s0[:, 0])
    o_b, s_b = _retention_chunked(q[:, ::-1], k[:, ::-1], v[:, ::-1], log_gamma[1], s0[:, 1])
    o = _head_rms(o_f + o_b[:, ::-1]) * norm_g.reshape(h_b, dv_b)
    y = jax.nn.silu(g) * o.reshape(bsz, s, h_b * dv_b)
    return y, jnp.stack([s_f, s_b], axis=1)


def _mlstm_chunked(q, k, v, i_pre, f_pre, c0, n0, m0):
    bsz, s, h, dk = q.shape
    dv = v.shape[-1]
    n = s // CHUNK
    q = q.reshape(bsz, n, CHUNK, h, dk)
    k = k.reshape(bsz, n, CHUNK, h, dk)
    v = v.reshape(bsz, n, CHUNK, h, dv)
    ig = i_pre.reshape(bsz, n, CHUNK, h)
    b = jnp.cumsum(jax.nn.log_sigmoid(f_pre).reshape(bsz, n, CHUNK, h), axis=2)
    b_tot = b[:, :, -1]
    a = b_tot[:, :, None] - b + ig
    m_chunk = jnp.max(a, axis=2)
    w = jnp.exp(a - m_chunk[:, :, None])
    kv = jnp.einsum('bnjh,bnjhd,bnjhe->bnhde', w, k, v)
    ks = jnp.einsum('bnjh,bnjhd->bnhd', w, k)

    def step(carry, inp):
        cm, nv, m = carry
        kv_c, ks_c, mc, bt = inp
        m_new = jnp.maximum(bt + m, mc)
        s_old = jnp.exp(bt + m - m_new)
        s_new = jnp.exp(mc - m_new)
        c_new = s_old[..., None, None] * cm + s_new[..., None, None] * kv_c
        n_new = s_old[..., None] * nv + s_new[..., None] * ks_c
        return (c_new, n_new, m_new), (cm, nv, m)

    xs = (jnp.moveaxis(kv, 1, 0), jnp.moveaxis(ks, 1, 0), jnp.moveaxis(m_chunk, 1, 0), jnp.moveaxis(b_tot, 1, 0))
    (c_f, n_f, m_f), (c_p, n_p, m_p) = lax.scan(step, (c0, n0, m0), xs)
    c_p = jnp.moveaxis(c_p, 0, 1)
    n_p = jnp.moveaxis(n_p, 0, 1)
    m_p = jnp.moveaxis(m_p, 0, 1)
    idx = jnp.arange(CHUNK)
    causal = (idx[:, None] >= idx[None, :])[None, None, :, :, None]
    dlog = jnp.where(causal, b[:, :, :, None, :] - b[:, :, None, :, :] + ig[:, :, None, :, :], -jnp.inf)
    g = b + m_p[:, :, None, :]
    m_i = jnp.maximum(g, jnp.max(dlog, axis=3))
    sc = jnp.einsum('bnihd,bnjhd->bnijh', q, k) * jnp.exp(dlog - m_i[:, :, :, None, :])
    sg = jnp.exp(g - m_i)
    num = jnp.einsum('bnijh,bnjhe->bnihe', sc, v) + jnp.einsum('bnihd,bnhde->bnihe', q, c_p) * sg[..., None]
    den = jnp.sum(sc, axis=3) + jnp.einsum('bnihd,bnhd->bnih', q, n_p) * sg
    hout = num / jnp.maximum(jnp.abs(den), jnp.exp(-m_i))[..., None]
    return hout.reshape(bsz, s, h, dv), c_f, n_f, m_f


def _mlstm_mixer(q, k, v, o, gts, gate_b, norm_g, c0, n0, m0):
    bsz, s, _ = q.shape
    _, _, h_c, dk_c, dv_c = c0.shape
    q = q.reshape(bsz, s, h_c, dk_c) * dk_c ** -0.5
    k = k.reshape(bsz, s, h_c, dk_c)
    v = v.reshape(bsz, s, h_c, dv_c)
    gp = (gts + gate_b).reshape(bsz, s, 4, h_c)
    h_f, cf, nf, mf = _mlstm_chunked(q, k, v, gp[:, :, 0], gp[:, :, 1], c0[:, 0], n0[:, 0], m0[:, 0])
    h_b, cb, nb, mb = _mlstm_chunked(q[:, ::-1], k[:, ::-1], v[:, ::-1], gp[:, ::-1, 2], gp[:, ::-1, 3],
                                     c0[:, 1], n0[:, 1], m0[:, 1])
    hh = _head_rms(h_f + h_b[:, ::-1]) * norm_g.reshape(h_c, dv_c)
    y = jax.nn.sigmoid(o) * hh.reshape(bsz, s, h_c * dv_c)
    return y, jnp.stack([cf, cb], axis=1), jnp.stack([nf, nb], axis=1), jnp.stack([mf, mb], axis=1)


def _hier_moe(x, w_group, w_router, w_gu, w_d, layer):
    bsz, s, d = x.shape
    t = bsz * s
    xt = x.reshape(t, d)
    g_logits = xt @ w_group
    g_prob = jax.nn.softmax(g_logits, axis=-1)
    g_idx = jnp.argmax(g_logits, axis=-1)
    p_g = jnp.take_along_axis(g_prob, g_idx[:, None], axis=1)[:, 0]
    e_logits = (xt @ w_router).reshape(t, N_GROUPS, E_PER_GROUP)
    e_logits = jnp.take_along_axis(e_logits, g_idx[:, None, None], axis=1)[:, 0]
    top_p, top_i = lax.top_k(jax.nn.softmax(e_logits, axis=-1), TOP_K)
    weights = p_g[:, None] * top_p / jnp.sum(top_p, axis=-1, keepdims=True)
    expert = g_idx[:, None] * E_PER_GROUP + top_i
    blk = EXPERT_TM
    n_asg = t * TOP_K
    flat_e = expert.reshape(n_asg)
    flat_t = jnp.repeat(jnp.arange(t), TOP_K)
    flat_w = weights.reshape(n_asg)
    order = jnp.argsort(flat_e)
    se = flat_e[order]
    st = flat_t[order]
    sw = flat_w[order]
    counts = jnp.bincount(flat_e, length=N_EXPERTS)
    start = jnp.cumsum(counts) - counts
    pcounts = (counts + blk - 1) // blk * blk
    pend = jnp.cumsum(pcounts)
    pstart = pend - pcounts
    dest = pstart[se] + jnp.arange(n_asg) - start[se]
    n_blocks = -(-n_asg // blk) + N_EXPERTS
    n_rows = n_blocks * blk
    row_tok = jnp.zeros((n_rows,), jnp.int32).at[dest].set(st.astype(jnp.int32))
    row_valid = jnp.zeros((n_rows,), x.dtype).at[dest].set(1)
    block_e = jnp.minimum(jnp.searchsorted(pend, jnp.arange(n_blocks) * blk, side='right'),
                          N_EXPERTS - 1).astype(jnp.int32)
    xb = xt[row_tok] * row_valid[:, None]
    yb = moe_experts(xb, block_e, w_gu, w_d, layer)
    y = jnp.zeros((t, d), F32).at[st].add(yb[dest] * sw[:, None])
    return y.reshape(bsz, s, d)


def _run_stream(x, cvec, latent, rg_h0, ret_s0, ml_c0, ml_n0, ml_m0,
                ada_w, ada_b, norm1_g, norm2_g, in0_w, out0_w, rg_conv_w, rg_conv_b,
                rg_wr, rg_br, rg_wi, rg_bi, rg_lambda, ret_decay, ret_norm_g,
                in1_w, out1_w, ml_gate_b, ml_norm_g, moe_wg, moe_wr, wgu_bf16, wd_bf16, final_g):
    bsz, s, d = x.shape
    t = bsz * s
    depth = ada_w.shape[0]
    d_a = rg_conv_w.shape[-1]
    qk_b = ret_s0.shape[3] * ret_s0.shape[4]
    v_b = ret_s0.shape[3] * ret_s0.shape[5]
    qk_c = ml_c0.shape[3] * ml_c0.shape[4]
    v_c = ml_c0.shape[3] * ml_c0.shape[5]
    sc = jax.nn.silu(cvec)
    rg_new, ret_new, c_new, n_new, m_new = [], [], [], [], []
    for l in range(depth):
        mod = sc @ ada_w[l] + ada_b[l]
        sh1, sc1, g1, sh2, sc2, g2 = jnp.split(mod[:, None, :], 6, axis=-1)
        h = _modulated_norm(x, norm1_g[l], sh1, sc1)
        j = l // 2
        if l % 2 == 0:
            proj = pmatmul(h.reshape(t, d), in0_w[j]).reshape(bsz, s, -1)
            split0 = (d_a, 2 * d_a, 2 * d_a + qk_b, 2 * d_a + 2 * qk_b, 2 * d_a + 2 * qk_b + v_b)
            xa, ga, q, k, v, g = jnp.split(proj, split0, axis=-1)
            ya, st_a = _rglru_mixer(xa, ga, rg_conv_w[j], rg_conv_b[j], rg_wr[j], rg_br[j],
                                    rg_wi[j], rg_bi[j], rg_lambda[j], rg_h0[:, j])
            yb, st_b = _retention_mixer(q, k, v, g, ret_decay[j], ret_norm_g[j], ret_s0[:, j], latent)
            cat = jnp.concatenate([ya, yb], axis=-1)
            mix = pmatmul(cat.reshape(t, -1), out0_w[j]).reshape(bsz, s, d)
            rg_new.append(st_a)
            ret_new.append(st_b)
        else:
            w_in = in1_w[j]
            n_main = 2 * qk_c + 2 * v_c
            n_gate = w_in.shape[1] - n_main
            w_gate = jnp.zeros((d, LANES), F32).at[:, :n_gate].set(w_in[:, n_main:])
            h2 = h.reshape(t, d)
            proj = pmatmul(h2, w_in[:, :n_main]).reshape(bsz, s, -1)
            gts = pmatmul(h2, w_gate, tn=LANES)[:, :n_gate].reshape(bsz, s, n_gate)
            q, k, v, o = jnp.split(proj, (qk_c, 2 * qk_c, 2 * qk_c + v_c), axis=-1)
            yc, st_c, st_n, st_m = _mlstm_mixer(q, k, v, o, gts, ml_gate_b[j], ml_norm_g[j],
                                                ml_c0[:, j], ml_n0[:, j], ml_m0[:, j])
            mix = pmatmul(yc.reshape(t, -1), out1_w[j]).reshape(bsz, s, d)
            c_new.append(st_c)
            n_new.append(st_n)
            m_new.append(st_m)
        x = x + g1 * mix
        h = _modulated_norm(x, norm2_g[l], sh2, sc2)
        x = x + g2 * _hier_moe(h, moe_wg[l], moe_wr[l], wgu_bf16, wd_bf16, l)
    y = x * lax.rsqrt(jnp.mean(x * x, axis=-1, keepdims=True) + NORM_EPS) * final_g
    return (y, jnp.stack(rg_new, axis=1), jnp.stack(ret_new, axis=1),
            jnp.stack(c_new, axis=1), jnp.stack(n_new, axis=1), jnp.stack(m_new, axis=1))


def kernel(x_prompt, x_sample, state_rglru_h, state_ret_s, state_mlstm_C, state_mlstm_n, state_mlstm_m, c, c_ctx, ada_w, ada_b, norm1_g, norm2_g, in0_w, out0_w, rg_conv_w, rg_conv_b, rg_wr, rg_br, rg_wi, rg_bi, rg_lambda, ret_decay, ret_norm_g, in1_w, out1_w, ml_gate_b, ml_norm_g, moe_wg, moe_wr, moe_wgu, moe_wd, final_g):
    bp = x_prompt.shape[0]
    weights = (ada_w, ada_b, norm1_g, norm2_g, in0_w, out0_w, rg_conv_w, rg_conv_b,
               rg_wr, rg_br, rg_wi, rg_bi, rg_lambda, ret_decay, ret_norm_g,
               in1_w, out1_w, ml_gate_b, ml_norm_g, moe_wg, moe_wr,
               moe_wgu.astype(BF16), moe_wd.astype(BF16), final_g)
    y_prompt, new_rglru_h, new_ret_s, new_mlstm_C, new_mlstm_n, new_mlstm_m = _run_stream(
        x_prompt, c_ctx[None, :], False,
        jnp.zeros((bp,) + state_rglru_h.shape[1:], F32),
        jnp.zeros((bp,) + state_ret_s.shape[1:], F32),
        jnp.zeros((bp,) + state_mlstm_C.shape[1:], F32),
        jnp.zeros((bp,) + state_mlstm_n.shape[1:], F32),
        jnp.zeros((bp,) + state_mlstm_m.shape[1:], F32),
        *weights)
    y_sample = _run_stream(x_sample, c, True, state_rglru_h, state_ret_s, state_mlstm_C,
                           state_mlstm_n, state_mlstm_m, *weights)[0]
    return (y_prompt, y_sample, new_rglru_h, new_ret_s, new_mlstm_C, new_mlstm_n, new_mlstm_m)
```

```python
import functools

import jax
import jax.numpy as jnp
from jax import lax
from jax.experimental import pallas as pl
from jax.experimental.pallas import tpu as pltpu

F32 = jnp.float32
BF16 = jnp.bfloat16

GRID_W = 64
CHUNK = 128
NORM_EPS = 1e-6
NB_A = 8
CONV_W = 4
CONV_LEFT = 2
RG_C = 8.0
H_B = 4
ROPE_BASE = 10000.0
H_C = 8
N_GROUPS = 4
E_PER_GROUP = 8
N_EXPERTS = N_GROUPS * E_PER_GROUP
TOP_K = 2
VMEM_LIMIT = 56 * 1024 * 1024
MATMUL_TM = 512
MATMUL_TN = 512
EXPERT_TM = 256
EXPERT_TE = 256
LANES = 128
SCAN_TC = 256
SCAN_UNROLL = 8


def _mm_kernel(a_ref, w_ref, o_ref):
    o_ref[...] = jnp.dot(a_ref[...].astype(BF16), w_ref[...].astype(BF16), preferred_element_type=F32)


def pmatmul(a, w, tn=MATMUL_TN):
    m, k = a.shape
    n = w.shape[1]
    tm = min(MATMUL_TM, m)
    tn = min(tn, n)
    assert m % tm == 0 and n % tn == 0
    return pl.pallas_call(
        _mm_kernel,
        grid=(m // tm, n // tn),
        in_specs=[pl.BlockSpec((tm, k), lambda i, j: (i, 0)),
                  pl.BlockSpec((k, tn), lambda i, j: (0, j))],
        out_specs=pl.BlockSpec((tm, tn), lambda i, j: (i, j)),
        out_shape=jax.ShapeDtypeStruct((m, n), F32),
        compiler_params=pltpu.CompilerParams(dimension_semantics=("arbitrary", "arbitrary"),
                                             vmem_limit_bytes=VMEM_LIMIT),
        name="proj_matmul",
    )(a, w)


def _expert_kernel(be_ref, x_ref, wg_ref, wu_ref, wd_ref, o_ref, xb_ref):
    j = pl.program_id(1)

    @pl.when(j == 0)
    def _():
        xb_ref[...] = x_ref[...].astype(BF16)

    xb = xb_ref[...]
    gate = jnp.dot(xb, wg_ref[...], preferred_element_type=F32)
    up = jnp.dot(xb, wu_ref[...], preferred_element_type=F32)
    act = ((gate * jax.nn.sigmoid(gate)) * up).astype(BF16)
    part = jnp.dot(act, wd_ref[...], preferred_element_type=F32)

    @pl.when(j == 0)
    def _():
        o_ref[...] = part

    @pl.when(j > 0)
    def _():
        o_ref[...] += part


def moe_experts(xb, block_e, w_gu, w_d, layer):
    r, d = xb.shape
    f = w_d.shape[2]
    tm, te = EXPERT_TM, EXPERT_TE
    assert r % tm == 0 and f % te == 0
    nj = f // te
    grid_spec = pltpu.PrefetchScalarGridSpec(
        num_scalar_prefetch=1,
        grid=(r // tm, nj),
        in_specs=[pl.BlockSpec((tm, d), lambda b, j, be: (b, 0)),
                  pl.BlockSpec((None, None, d, te), lambda b, j, be: (layer, be[b], 0, j)),
                  pl.BlockSpec((None, None, d, te), lambda b, j, be: (layer, be[b], 0, nj + j)),
                  pl.BlockSpec((None, None, te, d), lambda b, j, be: (layer, be[b], j, 0))],
        out_specs=pl.BlockSpec((tm, d), lambda b, j, be: (b, 0)),
        scratch_shapes=[pltpu.VMEM((tm, d), BF16)],
    )
    return pl.pallas_call(
        _expert_kernel,
        grid_spec=grid_spec,
        out_shape=jax.ShapeDtypeStruct((r, d), F32),
        compiler_params=pltpu.CompilerParams(dimension_semantics=("arbitrary", "arbitrary"),
                                             vmem_limit_bytes=VMEM_LIMIT),
        name="moe_experts",
    )(block_e, xb, w_gu, w_gu, w_d)


def _lscan_kernel(a_ref, u_ref, h0_ref, o_ref, *, reverse):
    s = a_ref.shape[0]

    def body(i, h):
        for r in range(SCAN_UNROLL):
            t = i * SCAN_UNROLL + r
            idx = (s - 1 - t) if reverse else t
            h = a_ref[pl.ds(idx, 1), :] * h + u_ref[pl.ds(idx, 1), :]
            o_ref[pl.ds(idx, 1), :] = h
        return h

    lax.fori_loop(0, s // SCAN_UNROLL, body, h0_ref[...])


def linear_scan(a, u, h0, reverse):
    b, s, c = a.shape
    tc = min(SCAN_TC, c)
    assert c % tc == 0 and s % SCAN_UNROLL == 0
    seq = pl.BlockSpec((None, s, tc), lambda i, j: (i, 0, j))
    return pl.pallas_call(
        functools.partial(_lscan_kernel, reverse=reverse),
        grid=(b, c // tc),
        in_specs=[seq, seq, pl.BlockSpec((None, 1, tc), lambda i, j: (i, 0, j))],
        out_specs=seq,
        out_shape=jax.ShapeDtypeStruct((b, s, c), F32),
        compiler_params=pltpu.CompilerParams(dimension_semantics=("arbitrary", "arbitrary"),
                                             vmem_limit_bytes=VMEM_LIMIT),
        name="linear_scan",
    )(a, u, h0.reshape(b, 1, c))


def _modulated_norm(x, g, shift, scale):
    xn = x * lax.rsqrt(jnp.mean(x * x, axis=-1, keepdims=True) + NORM_EPS) * g
    return xn * (1.0 + scale) + shift


def _head_rms(o):
    return o * lax.rsqrt(jnp.mean(o * o, axis=-1, keepdims=True) + NORM_EPS)


def _short_conv(x, w, b):
    s = x.shape[1]
    xp = jnp.pad(x, ((0, 0), (CONV_LEFT, CONV_W - 1 - CONV_LEFT), (0, 0)))
    return sum(xp[:, t:t + s] * w[t] for t in range(CONV_W)) + b


def _axial_rope(x):
    s = x.shape[1]
    rows = s // GRID_W
    row_id = jnp.repeat(jnp.arange(rows), GRID_W).astype(F32)
    col_id = (jnp.arange(rows * GRID_W) % GRID_W).astype(F32)
    n_f = x.shape[-1] // 4
    freqs = ROPE_BASE ** (-jnp.arange(n_f, dtype=F32) / n_f)
    ang = jnp.concatenate([row_id[:, None] * freqs, col_id[:, None] * freqs], axis=-1)
    cos = jnp.cos(ang)[None, :, None, :]
    sin = jnp.sin(ang)[None, :, None, :]
    x1 = x[..., 0::2]
    x2 = x[..., 1::2]
    return jnp.stack([x1 * cos - x2 * sin, x1 * sin + x2 * cos], axis=-1).reshape(x.shape)


def _rglru_mixer(xa, ga, conv_w, conv_b, wr, br, wi, bi, lam, h0):
    d_a = xa.shape[-1]
    bs_a = d_a // NB_A
    xc = _short_conv(xa, conv_w, conv_b)
    bsz, s, _ = xc.shape
    xblk = xc.reshape(bsz, s, NB_A, bs_a)
    r = jax.nn.sigmoid(jnp.einsum('bsnk,dnkj->dbsnj', xblk, wr).reshape(2, bsz, s, d_a) + br[:, None, None])
    gi = jax.nn.sigmoid(jnp.einsum('bsnk,dnkj->dbsnj', xblk, wi).reshape(2, bsz, s, d_a) + bi[:, None, None])
    log_a = -RG_C * r * jax.nn.softplus(-lam)[:, None, None]
    a = jnp.exp(log_a)
    u = jnp.sqrt(-jnp.expm1(2.0 * log_a)) * gi * xc[None]
    h_f = linear_scan(a[0], u[0], h0[:, 0], reverse=False)
    h_b = linear_scan(a[1], u[1], h0[:, 1], reverse=True)
    y = (h_f + h_b) * jax.nn.gelu(ga)
    return y, jnp.stack([h_f[:, -1], h_b[:, 0]], axis=1)


def _retention_chunked(q, k, v, log_gamma, s0):
    bsz, s, h, dk = q.shape
    dv = v.shape[-1]
    n = s // CHUNK
    q = q.reshape(bsz, n, CHUNK, h, dk)
    k = k.reshape(bsz, n, CHUNK, h, dk)
    v = v.reshape(bsz, n, CHUNK, h, dv)
    idx = jnp.arange(CHUNK, dtype=F32)
    diff = idx[:, None] - idx[None, :]
    decay_mask = jnp.where(diff[None] >= 0,
                           jnp.exp(jnp.maximum(diff, 0.0)[None] * log_gamma[:, None, None]), 0.0)
    scores = jnp.einsum('bnihd,bnjhd->bnhij', q, k) * decay_mask
    intra = jnp.einsum('bnhij,bnjhe->bnihe', scores, v)
    k_dec = jnp.exp((CHUNK - 1 - idx)[:, None] * log_gamma)
    kv = jnp.einsum('bnjhd,jh,bnjhe->bnhde', k, k_dec, v)
    chunk_dec = jnp.exp(CHUNK * log_gamma)[:, None, None]

    def step(st, kv_c):
        return chunk_dec * st + kv_c, st

    s_last, s_prev = lax.scan(step, s0, jnp.moveaxis(kv, 1, 0))
    s_prev = jnp.moveaxis(s_prev, 0, 1)
    q_dec = jnp.exp((idx + 1.0)[:, None] * log_gamma)
    inter = jnp.einsum('bnihd,ih,bnhde->bnihe', q, q_dec, s_prev)
    return (intra + inter).reshape(bsz, s, h, dv), s_last


def _retention_mixer(q, k, v, g, decay_p, norm_g, s0, latent):
    bsz, s, _ = q.shape
    _, _, h_b, dk_b, dv_b = s0.shape
    q = q.reshape(bsz, s, h_b, dk_b)
    k = k.reshape(bsz, s, h_b, dk_b) * dk_b ** -0.5
    v = v.reshape(bsz, s, h_b, dv_b)
    if latent:
        q = _axial_rope(q)
        k = _axial_rope(k)
    log_gamma = jnp.log1p(-jnp.exp(decay_p))
    o_f, s_f = _retention_chunked(q, k, v, log_gamma[0], s0[:, 0])
    o_b, s_b = _retention_chunked(q[:, ::-1], k[:, ::-1], v[:, ::-1], log_gamma[1], s0[:, 1])
    o = _head_rms(o_f + o_b[:, ::-1]) * norm_g.reshape(h_b, dv_b)
    y = jax.nn.silu(g) * o.reshape(bsz, s, h_b * dv_b)
    return y, jnp.stack([s_f, s_b], axis=1)


def _mlstm_chunked(q, k, v, i_pre, f_pre, c0, n0, m0):
    bsz, s, h, dk = q.shape
    dv = v.shape[-1]
    n = s // CHUNK
    q = q.reshape(bsz, n, CHUNK, h, dk)
    k = k.reshape(bsz, n, CHUNK, h, dk)
    v = v.reshape(bsz, n, CHUNK, h, dv)
    ig = i_pre.reshape(bsz, n, CHUNK, h)
    b = jnp.cumsum(jax.nn.log_sigmoid(f_pre).reshape(bsz, n, CHUNK, h), axis=2)
    b_tot = b[:, :, -1]
    a = b_tot[:, :, None] - b + ig
    m_chunk = jnp.max(a, axis=2)
    w = jnp.exp(a - m_chunk[:, :, None])
    kv = jnp.einsum('bnjh,bnjhd,bnjhe->bnhde', w, k, v)
    ks = jnp.einsum('bnjh,bnjhd->bnhd', w, k)

    def step(carry, inp):
        cm, nv, m = carry
        kv_c, ks_c, mc, bt = inp
        m_new = jnp.maximum(bt + m, mc)
        s_old = jnp.exp(bt + m - m_new)
        s_new = jnp.exp(mc - m_new)
        c_new = s_old[..., None, None] * cm + s_new[..., None, None] * kv_c
        n_new = s_old[..., None] * nv + s_new[..., None] * ks_c
        return (c_new, n_new, m_new), (cm, nv, m)

    xs = (jnp.moveaxis(kv, 1, 0), jnp.moveaxis(ks, 1, 0), jnp.moveaxis(m_chunk, 1, 0), jnp.moveaxis(b_tot, 1, 0))
    (c_f, n_f, m_f), (c_p, n_p, m_p) = lax.scan(step, (c0, n0, m0), xs)
    c_p = jnp.moveaxis(c_p, 0, 1)
    n_p = jnp.moveaxis(n_p, 0, 1)
    m_p = jnp.moveaxis(m_p, 0, 1)
    idx = jnp.arange(CHUNK)
    causal = (idx[:, None] >= idx[None, :])[None, None, :, :, None]
    dlog = jnp.where(causal, b[:, :, :, None, :] - b[:, :, None, :, :] + ig[:, :, None, :, :], -jnp.inf)
    g = b + m_p[:, :, None, :]
    m_i = jnp.maximum(g, jnp.max(dlog, axis=3))
    sc = jnp.einsum('bnihd,bnjhd->bnijh', q, k) * jnp.exp(dlog - m_i[:, :, :, None, :])
    sg = jnp.exp(g - m_i)
    num = jnp.einsum('bnijh,bnjhe->bnihe', sc, v) + jnp.einsum('bnihd,bnhde->bnihe', q, c_p) * sg[..., None]
    den = jnp.sum(sc, axis=3) + jnp.einsum('bnihd,bnhd->bnih', q, n_p) * sg
    hout = num / jnp.maximum(jnp.abs(den), jnp.exp(-m_i))[..., None]
    return hout.reshape(bsz, s, h, dv), c_f, n_f, m_f


def _mlstm_mixer(q, k, v, o, gts, gate_b, norm_g, c0, n0, m0):
    bsz, s, _ = q.shape
    _, _, h_c, dk_c, dv_c = c0.shape
    q = q.reshape(bsz, s, h_c, dk_c) * dk_c ** -0.5
    k = k.reshape(bsz, s, h_c, dk_c)
    v = v.reshape(bsz, s, h_c, dv_c)
    gp = (gts + gate_b).reshape(bsz, s, 4, h_c)
    h_f, cf, nf, mf = _mlstm_chunked(q, k, v, gp[:, :, 0], gp[:, :, 1], c0[:, 0], n0[:, 0], m0[:, 0])
    h_b, cb, nb, mb = _mlstm_chunked(q[:, ::-1], k[:, ::-1], v[:, ::-1], gp[:, ::-1, 2], gp[:, ::-1, 3],
                                     c0[:, 1], n0[:, 1], m0[:, 1])
    hh = _head_rms(h_f + h_b[:, ::-1]) * norm_g.reshape(h_c, dv_c)
    y = jax.nn.sigmoid(o) * hh.reshape(bsz, s, h_c * dv_c)
    return y, jnp.stack([cf, cb], axis=1), jnp.stack([nf, nb], axis=1), jnp.stack([mf, mb], axis=1)


def _hier_moe(x, w_group, w_router, w_gu, w_d, layer):
    bsz, s, d = x.shape
    t = bsz * s
    xt = x.reshape(t, d)
    g_logits = xt @ w_group
    g_prob = jax.nn.softmax(g_logits, axis=-1)
    g_idx = jnp.argmax(g_logits, axis=-1)
    p_g = jnp.take_along_axis(g_prob, g_idx[:, None], axis=1)[:, 0]
    e_logits = (xt @ w_router).reshape(t, N_GROUPS, E_PER_GROUP)
    e_logits = jnp.take_along_axis(e_logits, g_idx[:, None, None], axis=1)[:, 0]
    top_p, top_i = lax.top_k(jax.nn.softmax(e_logits, axis=-1), TOP_K)
    weights = p_g[:, None] * top_p / jnp.sum(top_p, axis=-1, keepdims=True)
    expert = g_idx[:, None] * E_PER_GROUP + top_i
    blk = EXPERT_TM
    n_asg = t * TOP_K
    flat_e = expert.reshape(n_asg)
    flat_t = jnp.repeat(jnp.arange(t), TOP_K)
    flat_w = weights.reshape(n_asg)
    order = jnp.argsort(flat_e)
    se = flat_e[order]
    st = flat_t[order]
    sw = flat_w[order]
    counts = jnp.bincount(flat_e, length=N_EXPERTS)
    start = jnp.cumsum(counts) - counts
    pcounts = (counts + blk - 1) // blk * blk
    pend = jnp.cumsum(pcounts)
    pstart = pend - pcounts
    dest = pstart[se] + jnp.arange(n_asg) - start[se]
    n_blocks = -(-n_asg // blk) + N_EXPERTS
    n_rows = n_blocks * blk
    row_tok = jnp.zeros((n_rows,), jnp.int32).at[dest].set(st.astype(jnp.int32))
    row_valid = jnp.zeros((n_rows,), x.dtype).at[dest].set(1)
    block_e = jnp.minimum(jnp.searchsorted(pend, jnp.arange(n_blocks) * blk, side='right'),
                          N_EXPERTS - 1).astype(jnp.int32)
    xb = xt[row_tok] * row_valid[:, None]
    yb = moe_experts(xb, block_e, w_gu, w_d, layer)
    y = jnp.zeros((t, d), F32).at[st].add(yb[dest] * sw[:, None])
    return y.reshape(bsz, s, d)


def _run_stream(x, cvec, latent, rg_h0, ret_s0, ml_c0, ml_n0, ml_m0,
                ada_w, ada_b, norm1_g, norm2_g, in0_w, out0_w, rg_conv_w, rg_conv_b,
                rg_wr, rg_br, rg_wi, rg_bi, rg_lambda, ret_decay, ret_norm_g,
                in1_w, out1_w, ml_gate_b, ml_norm_g, moe_wg, moe_wr, wgu_bf16, wd_bf16, final_g):
    bsz, s, d = x.shape
    t = bsz * s
    depth = ada_w.shape[0]
    d_a = rg_conv_w.shape[-1]
    qk_b = ret_s0.shape[3] * ret_s0.shape[4]
    v_b = ret_s0.shape[3] * ret_s0.shape[5]
    qk_c = ml_c0.shape[3] * ml_c0.shape[4]
    v_c = ml_c0.shape[3] * ml_c0.shape[5]
    sc = jax.nn.silu(cvec)
    rg_new, ret_new, c_new, n_new, m_new = [], [], [], [], []
    for l in range(depth):
        mod = sc @ ada_w[l] + ada_b[l]
        sh1, sc1, g1, sh2, sc2, g2 = jnp.split(mod[:, None, :], 6, axis=-1)
        h = _modulated_norm(x, norm1_g[l], sh1, sc1)
        j = l // 2
        if l % 2 == 0:
            proj = pmatmul(h.reshape(t, d), in0_w[j]).reshape(bsz, s, -1)
            split0 = (d_a, 2 * d_a, 2 * d_a + qk_b, 2 * d_a + 2 * qk_b, 2 * d_a + 2 * qk_b + v_b)
            xa, ga, q, k, v, g = jnp.split(proj, split0, axis=-1)
            ya, st_a = _rglru_mixer(xa, ga, rg_conv_w[j], rg_conv_b[j], rg_wr[j], rg_br[j],
                                    rg_wi[j], rg_bi[j], rg_lambda[j], rg_h0[:, j])
            yb, st_b = _retention_mixer(q, k, v, g, ret_decay[j], ret_norm_g[j], ret_s0[:, j], latent)
            cat = jnp.concatenate([ya, yb], axis=-1)
            mix = pmatmul(cat.reshape(t, -1), out0_w[j]).reshape(bsz, s, d)
            rg_new.append(st_a)
            ret_new.append(st_b)
        else:
            w_in = in1_w[j]
            n_main = 2 * qk_c + 2 * v_c
            n_gate = w_in.shape[1] - n_main
            w_gate = jnp.zeros((d, LANES), F32).at[:, :n_gate].set(w_in[:, n_main:])
            h2 = h.reshape(t, d)
            proj = pmatmul(h2, w_in[:, :n_main]).reshape(bsz, s, -1)
            gts = pmatmul(h2, w_gate, tn=LANES)[:, :n_gate].reshape(bsz, s, n_gate)
            q, k, v, o = jnp.split(proj, (qk_c, 2 * qk_c, 2 * qk_c + v_c), axis=-1)
            yc, st_c, st_n, st_m = _mlstm_mixer(q, k, v, o, gts, ml_gate_b[j], ml_norm_g[j],
                                                ml_c0[:, j], ml_n0[:, j], ml_m0[:, j])
            mix = pmatmul(yc.reshape(t, -1), out1_w[j]).reshape(bsz, s, d)
            c_new.append(st_c)
            n_new.append(st_n)
            m_new.append(st_m)
        x = x + g1 * mix
        h = _modulated_norm(x, norm2_g[l], sh2, sc2)
        x = x + g2 * _hier_moe(h, moe_wg[l], moe_wr[l], wgu_bf16, wd_bf16, l)
    y = x * lax.rsqrt(jnp.mean(x * x, axis=-1, keepdims=True) + NORM_EPS) * final_g
    return (y, jnp.stack(rg_new, axis=1), jnp.stack(ret_new, axis=1),
            jnp.stack(c_new, axis=1), jnp.stack(n_new, axis=1), jnp.stack(m_new, axis=1))


def kernel(x_prompt, x_sample, state_rglru_h, state_ret_s, state_mlstm_C, state_mlstm_n, state_mlstm_m, c, c_ctx, ada_w, ada_b, norm1_g, norm2_g, in0_w, out0_w, rg_conv_w, rg_conv_b, rg_wr, rg_br, rg_wi, rg_bi, rg_lambda, ret_decay, ret_norm_g, in1_w, out1_w, ml_gate_b, ml_norm_g, moe_wg, moe_wr, moe_wgu, moe_wd, final_g):
    bp = x_prompt.shape[0]
    weights = (ada_w, ada_b, norm1_g, norm2_g, in0_w, out0_w, rg_conv_w, rg_conv_b,
               rg_wr, rg_br, rg_wi, rg_bi, rg_lambda, ret_decay, ret_norm_g,
               in1_w, out1_w, ml_gate_b, ml_norm_g, moe_wg, moe_wr,
               moe_wgu.astype(BF16), moe_wd.astype(BF16), final_g)
    y_prompt, new_rglru_h, new_ret_s, new_mlstm_C, new_mlstm_n, new_mlstm_m = _run_stream(
        x_prompt, c_ctx[None, :], False,
        jnp.zeros((bp,) + state_rglru_h.shape[1:], F32),
        jnp.zeros((bp,) + state_ret_s.shape[1:], F32),
        jnp.zeros((bp,) + state_mlstm_C.shape[1:], F32),
        jnp.zeros((bp,) + state_mlstm_n.shape[1:], F32),
        jnp.zeros((bp,) + state_mlstm_m.shape[1:], F32),
        *weights)
    y_sample = _run_stream(x_sample, c, True, state_rglru_h, state_ret_s, state_mlstm_C,
                           state_mlstm_n, state_mlstm_m, *weights)[0]
    return (y_prompt, y_sample, new_rglru_h, new_ret_s, new_mlstm_C, new_mlstm_n, new_mlstm_m)
```

```python
import functools

import jax
import jax.numpy as jnp
from jax import lax
from jax.experimental import pallas as pl
from jax.experimental.pallas import tpu as pltpu

F32 = jnp.float32
BF16 = jnp.bfloat16

GRID_W = 64
CHUNK = 128
NORM_EPS = 1e-6
NB_A = 8
CONV_W = 4
CONV_LEFT = 2
RG_C = 8.0
H_B = 4
ROPE_BASE = 10000.0
H_C = 8
N_GROUPS = 4
E_PER_GROUP = 8
N_EXPERTS = N_GROUPS * E_PER_GROUP
TOP_K = 2
VMEM_LIMIT = 56 * 1024 * 1024
MATMUL_TM = 512
MATMUL_TN = 512
EXPERT_TM = 512
EXPERT_TE = 256
LANES = 128
SCAN_TC = 256
SCAN_UNROLL = 8


def _mm_kernel(a_ref, w_ref, o_ref):
    o_ref[...] = jnp.dot(a_ref[...].astype(BF16), w_ref[...].astype(BF16), preferred_element_type=F32)


def pmatmul(a, w, tn=MATMUL_TN):
    m, k = a.shape
    n = w.shape[1]
    tm = min(MATMUL_TM, m)
    tn = min(tn, n)
    assert m % tm == 0 and n % tn == 0
    return pl.pallas_call(
        _mm_kernel,
        grid=(m // tm, n // tn),
        in_specs=[pl.BlockSpec((tm, k), lambda i, j: (i, 0)),
                  pl.BlockSpec((k, tn), lambda i, j: (0, j))],
        out_specs=pl.BlockSpec((tm, tn), lambda i, j: (i, j)),
        out_shape=jax.ShapeDtypeStruct((m, n), F32),
        compiler_params=pltpu.CompilerParams(dimension_semantics=("arbitrary", "arbitrary"),
                                             vmem_limit_bytes=VMEM_LIMIT),
        name="proj_matmul",
    )(a, w)


def _expert_kernel(be_ref, x_ref, wg_ref, wu_ref, wd_ref, o_ref, xb_ref):
    j = pl.program_id(1)

    @pl.when(j == 0)
    def _():
        xb_ref[...] = x_ref[...].astype(BF16)

    xb = xb_ref[...]
    gate = jnp.dot(xb, wg_ref[...].astype(BF16), preferred_element_type=F32)
    up = jnp.dot(xb, wu_ref[...].astype(BF16), preferred_element_type=F32)
    act = ((gate * jax.nn.sigmoid(gate)) * up).astype(BF16)
    part = jnp.dot(act, wd_ref[...].astype(BF16), preferred_element_type=F32)

    @pl.when(j == 0)
    def _():
        o_ref[...] = part

    @pl.when(j > 0)
    def _():
        o_ref[...] += part


def moe_experts(xb, block_e, w_gu, w_d, layer):
    r, d = xb.shape
    f = w_d.shape[2]
    tm, te = EXPERT_TM, EXPERT_TE
    assert r % tm == 0 and f % te == 0
    nj = f // te
    grid_spec = pltpu.PrefetchScalarGridSpec(
        num_scalar_prefetch=1,
        grid=(r // tm, nj),
        in_specs=[pl.BlockSpec((tm, d), lambda b, j, be: (b, 0)),
                  pl.BlockSpec((None, None, d, te), lambda b, j, be: (layer, be[b], 0, j)),
                  pl.BlockSpec((None, None, d, te), lambda b, j, be: (layer, be[b], 0, nj + j)),
                  pl.BlockSpec((None, None, te, d), lambda b, j, be: (layer, be[b], j, 0))],
        out_specs=pl.BlockSpec((tm, d), lambda b, j, be: (b, 0)),
        scratch_shapes=[pltpu.VMEM((tm, d), BF16)],
    )
    return pl.pallas_call(
        _expert_kernel,
        grid_spec=grid_spec,
        out_shape=jax.ShapeDtypeStruct((r, d), F32),
        compiler_params=pltpu.CompilerParams(dimension_semantics=("arbitrary", "arbitrary"),
                                             vmem_limit_bytes=VMEM_LIMIT),
        name="moe_experts",
    )(block_e, xb, w_gu, w_gu, w_d)


def _lscan_kernel(a_ref, u_ref, h0_ref, o_ref, *, reverse):
    s = a_ref.shape[0]

    def body(i, h):
        for r in range(SCAN_UNROLL):
            t = i * SCAN_UNROLL + r
            idx = (s - 1 - t) if reverse else t
            h = a_ref[pl.ds(idx, 1), :] * h + u_ref[pl.ds(idx, 1), :]
            o_ref[pl.ds(idx, 1), :] = h
        return h

    lax.fori_loop(0, s // SCAN_UNROLL, body, h0_ref[...])


def linear_scan(a, u, h0, reverse):
    b, s, c = a.shape
    tc = min(SCAN_TC, c)
    assert c % tc == 0 and s % SCAN_UNROLL == 0
    seq = pl.BlockSpec((None, s, tc), lambda i, j: (i, 0, j))
    return pl.pallas_call(
        functools.partial(_lscan_kernel, reverse=reverse),
        grid=(b, c // tc),
        in_specs=[seq, seq, pl.BlockSpec((None, 1, tc), lambda i, j: (i, 0, j))],
        out_specs=seq,
        out_shape=jax.ShapeDtypeStruct((b, s, c), F32),
        compiler_params=pltpu.CompilerParams(dimension_semantics=("arbitrary", "arbitrary"),
                                             vmem_limit_bytes=VMEM_LIMIT),
        name="linear_scan",
    )(a, u, h0.reshape(b, 1, c))


def _modulated_norm(x, g, shift, scale):
    xn = x * lax.rsqrt(jnp.mean(x * x, axis=-1, keepdims=True) + NORM_EPS) * g
    return xn * (1.0 + scale) + shift


def _head_rms(o):
    return o * lax.rsqrt(jnp.mean(o * o, axis=-1, keepdims=True) + NORM_EPS)


def _short_conv(x, w, b):
    s = x.shape[1]
    xp = jnp.pad(x, ((0, 0), (CONV_LEFT, CONV_W - 1 - CONV_LEFT), (0, 0)))
    return sum(xp[:, t:t + s] * w[t] for t in range(CONV_W)) + b


def _axial_rope(x):
    s = x.shape[1]
    rows = s // GRID_W
    row_id = jnp.repeat(jnp.arange(rows), GRID_W).astype(F32)
    col_id = (jnp.arange(rows * GRID_W) % GRID_W).astype(F32)
    n_f = x.shape[-1] // 4
    freqs = ROPE_BASE ** (-jnp.arange(n_f, dtype=F32) / n_f)
    ang = jnp.concatenate([row_id[:, None] * freqs, col_id[:, None] * freqs], axis=-1)
    cos = jnp.cos(ang)[None, :, None, :]
    sin = jnp.sin(ang)[None, :, None, :]
    x1 = x[..., 0::2]
    x2 = x[..., 1::2]
    return jnp.stack([x1 * cos - x2 * sin, x1 * sin + x2 * cos], axis=-1).reshape(x.shape)


def _rglru_mixer(xa, ga, conv_w, conv_b, wr, br, wi, bi, lam, h0):
    d_a = xa.shape[-1]
    bs_a = d_a // NB_A
    xc = _short_conv(xa, conv_w, conv_b)
    bsz, s, _ = xc.shape
    xblk = xc.reshape(bsz, s, NB_A, bs_a)
    def gates(d):
        r = jax.nn.sigmoid(jnp.einsum('bsnk,nkj->bsnj', xblk, wr[d]).reshape(bsz, s, d_a) + br[d])
        gi = jax.nn.sigmoid(jnp.einsum('bsnk,nkj->bsnj', xblk, wi[d]).reshape(bsz, s, d_a) + bi[d])
        log_a = -RG_C * r * jax.nn.softplus(-lam[d])
        return jnp.exp(log_a), jnp.sqrt(-jnp.expm1(2.0 * log_a)) * gi * xc

    a_f, u_f = gates(0)
    a_b, u_b = gates(1)
    h_f = linear_scan(a_f, u_f, h0[:, 0], reverse=False)
    h_b = linear_scan(a_b, u_b, h0[:, 1], reverse=True)
    y = (h_f + h_b) * jax.nn.gelu(ga)
    return y, jnp.stack([h_f[:, -1], h_b[:, 0]], axis=1)


def _retention_chunked(q, k, v, log_gamma, s0):
    bsz, s, h, dk = q.shape
    dv = v.shape[-1]
    n = s // CHUNK
    q = q.reshape(bsz, n, CHUNK, h, dk)
    k = k.reshape(bsz, n, CHUNK, h, dk)
    v = v.reshape(bsz, n, CHUNK, h, dv)
    idx = jnp.arange(CHUNK, dtype=F32)
    diff = idx[:, None] - idx[None, :]
    decay_mask = jnp.where(diff[None] >= 0,
                           jnp.exp(jnp.maximum(diff, 0.0)[None] * log_gamma[:, None, None]), 0.0)
    scores = jnp.einsum('bnihd,bnjhd->bnhij', q, k) * decay_mask
    intra = jnp.einsum('bnhij,bnjhe->bnihe', scores, v)
    k_dec = jnp.exp((CHUNK - 1 - idx)[:, None] * log_gamma)
    kv = jnp.einsum('bnjhd,jh,bnjhe->bnhde', k, k_dec, v)
    chunk_dec = jnp.exp(CHUNK * log_gamma)[:, None, None]

    def step(st, kv_c):
        return chunk_dec * st + kv_c, st

    s_last, s_prev = lax.scan(step, s0, jnp.moveaxis(kv, 1, 0))
    s_prev = jnp.moveaxis(s_prev, 0, 1)
    q_dec = jnp.exp((idx + 1.0)[:, None] * log_gamma)
    inter = jnp.einsum('bnihd,ih,bnhde->bnihe', q, q_dec, s_prev)
    return (intra + inter).reshape(bsz, s, h, dv), s_last


def _retention_mixer(q, k, v, g, decay_p, norm_g, s0, latent):
    bsz, s, _ = q.shape
    _, _, h_b, dk_b, dv_b = s0.shape
    q = q.reshape(bsz, s, h_b, dk_b)
    k = k.reshape(bsz, s, h_b, dk_b) * dk_b ** -0.5
    v = v.reshape(bsz, s, h_b, dv_b)
    if latent:
        q = _axial_rope(q)
        k = _axial_rope(k)
    log_gamma = jnp.log1p(-jnp.exp(decay_p))
    o_f, s_f = _retention_chunked(q, k, v, log_gamma[0], s0[:, 0])
    o_b, s_b = _retention_chunked(q[:, ::-1], k[:, ::-1], v[:, ::-1], log_gamma[1], s0[:, 1])
    o = _head_rms(o_f + o_b[:, ::-1]) * norm_g.reshape(h_b, dv_b)
    y = jax.nn.silu(g) * o.reshape(bsz, s, h_b * dv_b)
    return y, jnp.stack([s_f, s_b], axis=1)


def _mlstm_chunked(q, k, v, i_pre, f_pre, c0, n0, m0):
    bsz, s, h, dk = q.shape
    dv = v.shape[-1]
    n = s // CHUNK
    q = q.reshape(bsz, n, CHUNK, h, dk)
    k = k.reshape(bsz, n, CHUNK, h, dk)
    v = v.reshape(bsz, n, CHUNK, h, dv)
    ig = i_pre.reshape(bsz, n, CHUNK, h)
    b = jnp.cumsum(jax.nn.log_sigmoid(f_pre).reshape(bsz, n, CHUNK, h), axis=2)
    b_tot = b[:, :, -1]
    a = b_tot[:, :, None] - b + ig
    m_chunk = jnp.max(a, axis=2)
    w = jnp.exp(a - m_chunk[:, :, None])
    kv = jnp.einsum('bnjh,bnjhd,bnjhe->bnhde', w, k, v)
    ks = jnp.einsum('bnjh,bnjhd->bnhd', w, k)

    def step(carry, inp):
        cm, nv, m = carry
        kv_c, ks_c, mc, bt = inp
        m_new = jnp.maximum(bt + m, mc)
        s_old = jnp.exp(bt + m - m_new)
        s_new = jnp.exp(mc - m_new)
        c_new = s_old[..., None, None] * cm + s_new[..., None, None] * kv_c
        n_new = s_old[..., None] * nv + s_new[..., None] * ks_c
        return (c_new, n_new, m_new), (cm, nv, m)

    xs = (jnp.moveaxis(kv, 1, 0), jnp.moveaxis(ks, 1, 0), jnp.moveaxis(m_chunk, 1, 0), jnp.moveaxis(b_tot, 1, 0))
    (c_f, n_f, m_f), (c_p, n_p, m_p) = lax.scan(step, (c0, n0, m0), xs)
    c_p = jnp.moveaxis(c_p, 0, 1)
    n_p = jnp.moveaxis(n_p, 0, 1)
    m_p = jnp.moveaxis(m_p, 0, 1)
    idx = jnp.arange(CHUNK)
    causal = (idx[:, None] >= idx[None, :])[None, None, :, :, None]
    dlog = jnp.where(causal, b[:, :, :, None, :] - b[:, :, None, :, :] + ig[:, :, None, :, :], -jnp.inf)
    g = b + m_p[:, :, None, :]
    m_i = jnp.maximum(g, jnp.max(dlog, axis=3))
    sc = jnp.einsum('bnihd,bnjhd->bnijh', q, k) * jnp.exp(dlog - m_i[:, :, :, None, :])
    sg = jnp.exp(g - m_i)
    num = jnp.einsum('bnijh,bnjhe->bnihe', sc, v) + jnp.einsum('bnihd,bnhde->bnihe', q, c_p) * sg[..., None]
    den = jnp.sum(sc, axis=3) + jnp.einsum('bnihd,bnhd->bnih', q, n_p) * sg
    hout = num / jnp.maximum(jnp.abs(den), jnp.exp(-m_i))[..., None]
    return hout.reshape(bsz, s, h, dv), c_f, n_f, m_f


def _mlstm_mixer(q, k, v, o, gts, gate_b, norm_g, c0, n0, m0):
    bsz, s, _ = q.shape
    _, _, h_c, dk_c, dv_c = c0.shape
    q = q.reshape(bsz, s, h_c, dk_c) * dk_c ** -0.5
    k = k.reshape(bsz, s, h_c, dk_c)
    v = v.reshape(bsz, s, h_c, dv_c)
    gp = (gts + gate_b).reshape(bsz, s, 4, h_c)
    h_f, cf, nf, mf = _mlstm_chunked(q, k, v, gp[:, :, 0], gp[:, :, 1], c0[:, 0], n0[:, 0], m0[:, 0])
    h_b, cb, nb, mb = _mlstm_chunked(q[:, ::-1], k[:, ::-1], v[:, ::-1], gp[:, ::-1, 2], gp[:, ::-1, 3],
                                     c0[:, 1], n0[:, 1], m0[:, 1])
    hh = _head_rms(h_f + h_b[:, ::-1]) * norm_g.reshape(h_c, dv_c)
    y = jax.nn.sigmoid(o) * hh.reshape(bsz, s, h_c * dv_c)
    return y, jnp.stack([cf, cb], axis=1), jnp.stack([nf, nb], axis=1), jnp.stack([mf, mb], axis=1)


def _hier_moe(x, w_group, w_router, w_gu, w_d, layer):
    bsz, s, d = x.shape
    t = bsz * s
    xt = x.reshape(t, d)
    g_logits = xt @ w_group
    g_prob = jax.nn.softmax(g_logits, axis=-1)
    g_idx = jnp.argmax(g_logits, axis=-1)
    p_g = jnp.take_along_axis(g_prob, g_idx[:, None], axis=1)[:, 0]
    e_logits = (xt @ w_router).reshape(t, N_GROUPS, E_PER_GROUP)
    e_logits = jnp.take_along_axis(e_logits, g_idx[:, None, None], axis=1)[:, 0]
    top_p, top_i = lax.top_k(jax.nn.softmax(e_logits, axis=-1), TOP_K)
    weights = p_g[:, None] * top_p / jnp.sum(top_p, axis=-1, keepdims=True)
    expert = g_idx[:, None] * E_PER_GROUP + top_i
    blk = EXPERT_TM
    n_asg = t * TOP_K
    flat_e = expert.reshape(n_asg)
    flat_t = jnp.repeat(jnp.arange(t), TOP_K)
    flat_w = weights.reshape(n_asg)
    order = jnp.argsort(flat_e)
    se = flat_e[order]
    st = flat_t[order]
    sw = flat_w[order]
    counts = jnp.bincount(flat_e, length=N_EXPERTS)
    start = jnp.cumsum(counts) - counts
    pcounts = (counts + blk - 1) // blk * blk
    pend = jnp.cumsum(pcounts)
    pstart = pend - pcounts
    dest = pstart[se] + jnp.arange(n_asg) - start[se]
    n_blocks = -(-n_asg // blk) + N_EXPERTS
    n_rows = n_blocks * blk
    row_tok = jnp.zeros((n_rows,), jnp.int32).at[dest].set(st.astype(jnp.int32))
    row_valid = jnp.zeros((n_rows,), x.dtype).at[dest].set(1)
    block_e = jnp.minimum(jnp.searchsorted(pend, jnp.arange(n_blocks) * blk, side='right'),
                          N_EXPERTS - 1).astype(jnp.int32)
    xb = xt[row_tok] * row_valid[:, None]
    yb = moe_experts(xb, block_e, w_gu, w_d, layer)
    y = jnp.zeros((t, d), F32).at[st].add(yb[dest] * sw[:, None])
    return y.reshape(bsz, s, d)


def _run_stream(x, cvec, latent, rg_h0, ret_s0, ml_c0, ml_n0, ml_m0,
                ada_w, ada_b, norm1_g, norm2_g, in0_w, out0_w, rg_conv_w, rg_conv_b,
                rg_wr, rg_br, rg_wi, rg_bi, rg_lambda, ret_decay, ret_norm_g,
                in1_w, out1_w, ml_gate_b, ml_norm_g, moe_wg, moe_wr, moe_wgu, moe_wd, final_g):
    bsz, s, d = x.shape
    t = bsz * s
    depth = ada_w.shape[0]
    d_a = rg_conv_w.shape[-1]
    qk_b = ret_s0.shape[3] * ret_s0.shape[4]
    v_b = ret_s0.shape[3] * ret_s0.shape[5]
    qk_c = ml_c0.shape[3] * ml_c0.shape[4]
    v_c = ml_c0.shape[3] * ml_c0.shape[5]
    sc = jax.nn.silu(cvec)
    rg_new, ret_new, c_new, n_new, m_new = [], [], [], [], []
    for l in range(depth):
        mod = sc @ ada_w[l] + ada_b[l]
        sh1, sc1, g1, sh2, sc2, g2 = jnp.split(mod[:, None, :], 6, axis=-1)
        h = _modulated_norm(x, norm1_g[l], sh1, sc1)
        j = l // 2
        if l % 2 == 0:
            proj = pmatmul(h.reshape(t, d), in0_w[j]).reshape(bsz, s, -1)
            split0 = (d_a, 2 * d_a, 2 * d_a + qk_b, 2 * d_a + 2 * qk_b, 2 * d_a + 2 * qk_b + v_b)
            xa, ga, q, k, v, g = jnp.split(proj, split0, axis=-1)
            ya, st_a = _rglru_mixer(xa, ga, rg_conv_w[j], rg_conv_b[j], rg_wr[j], rg_br[j],
                                    rg_wi[j], rg_bi[j], rg_lambda[j], rg_h0[:, j])
            yb, st_b = _retention_mixer(q, k, v, g, ret_decay[j], ret_norm_g[j], ret_s0[:, j], latent)
            cat = jnp.concatenate([ya, yb], axis=-1)
            mix = pmatmul(cat.reshape(t, -1), out0_w[j]).reshape(bsz, s, d)
            rg_new.append(st_a)
            ret_new.append(st_b)
        else:
            w_in = in1_w[j]
            n_main = 2 * qk_c + 2 * v_c
            n_gate = w_in.shape[1] - n_main
            w_gate = jnp.zeros((d, LANES), F32).at[:, :n_gate].set(w_in[:, n_main:])
            h2 = h.reshape(t, d)
            proj = pmatmul(h2, w_in[:, :n_main]).reshape(bsz, s, -1)
            gts = pmatmul(h2, w_gate, tn=LANES)[:, :n_gate].reshape(bsz, s, n_gate)
            q, k, v, o = jnp.split(proj, (qk_c, 2 * qk_c, 2 * qk_c + v_c), axis=-1)
            yc, st_c, st_n, st_m = _mlstm_mixer(q, k, v, o, gts, ml_gate_b[j], ml_norm_g[j],
                                                ml_c0[:, j], ml_n0[:, j], ml_m0[:, j])
            mix = pmatmul(yc.reshape(t, -1), out1_w[j]).reshape(bsz, s, d)
            c_new.append(st_c)
            n_new.append(st_n)
            m_new.append(st_m)
        x = x + g1 * mix
        h = _modulated_norm(x, norm2_g[l], sh2, sc2)
        x = x + g2 * _hier_moe(h, moe_wg[l], moe_wr[l], moe_wgu, moe_wd, l)
    y = x * lax.rsqrt(jnp.mean(x * x, axis=-1, keepdims=True) + NORM_EPS) * final_g
    return (y, jnp.stack(rg_new, axis=1), jnp.stack(ret_new, axis=1),
            jnp.stack(c_new, axis=1), jnp.stack(n_new, axis=1), jnp.stack(m_new, axis=1))


def kernel(x_prompt, x_sample, state_rglru_h, state_ret_s, state_mlstm_C, state_mlstm_n, state_mlstm_m, c, c_ctx, ada_w, ada_b, norm1_g, norm2_g, in0_w, out0_w, rg_conv_w, rg_conv_b, rg_wr, rg_br, rg_wi, rg_bi, rg_lambda, ret_decay, ret_norm_g, in1_w, out1_w, ml_gate_b, ml_norm_g, moe_wg, moe_wr, moe_wgu, moe_wd, final_g):
    bp = x_prompt.shape[0]
    weights = (ada_w, ada_b, norm1_g, norm2_g, in0_w, out0_w, rg_conv_w, rg_conv_b,
               rg_wr, rg_br, rg_wi, rg_bi, rg_lambda, ret_decay, ret_norm_g,
               in1_w, out1_w, ml_gate_b, ml_norm_g, moe_wg, moe_wr,
               moe_wgu, moe_wd, final_g)
    y_prompt, new_rglru_h, new_ret_s, new_mlstm_C, new_mlstm_n, new_mlstm_m = _run_stream(
        x_prompt, c_ctx[None, :], False,
        jnp.zeros((bp,) + state_rglru_h.shape[1:], F32),
        jnp.zeros((bp,) + state_ret_s.shape[1:], F32),
        jnp.zeros((bp,) + state_mlstm_C.shape[1:], F32),
        jnp.zeros((bp,) + state_mlstm_n.shape[1:], F32),
        jnp.zeros((bp,) + state_mlstm_m.shape[1:], F32),
        *weights)
    y_sample = _run_stream(x_sample, c, True, state_rglru_h, state_ret_s, state_mlstm_C,
                           state_mlstm_n, state_mlstm_m, *weights)[0]
    return (y_prompt, y_sample, new_rglru_h, new_ret_s, new_mlstm_C, new_mlstm_n, new_mlstm_m)
```
